```python
import math
import jax, jax.numpy as jnp
from jax import lax
import numpy as np

D_MODEL = 1024
BATCH = 4
SEQ = 8192
DEPTH = 2

GRID_W = 64
CTX_LEN = 256
ROPE_THETA = 10000.0
NORM_EPS = 1e-6
NEG_INF = -1e30
Q_BLOCK = 128

HEAD_DIM = 64
A_HEADS = 8
A_KV_HEADS = 2
B_HEADS = 8
B_KV_HEADS = 2
WINDOW = 128
A_Q_W = A_HEADS * HEAD_DIM
A_KV_W = A_KV_HEADS * HEAD_DIM
B_Q_W = B_HEADS * HEAD_DIM
B_KV_W = B_KV_HEADS * HEAD_DIM
EVEN_WIDTHS = (A_Q_W, A_KV_W, A_KV_W, B_Q_W, B_KV_W, B_KV_W)
EVEN_IN_W = sum(EVEN_WIDTHS)
EVEN_OUT_W = (A_HEADS + B_HEADS) * HEAD_DIM

C_HEADS = 8
C_Q_LORA = 384
C_KV_LORA = 256
C_NOPE = 128
C_ROPE = 64
C_V = 128
C_IN_W = C_Q_LORA + C_KV_LORA + C_ROPE

PEER_HEADS = 8
PEER_KEYS = 128
PEER_EXPERTS = PEER_KEYS * PEER_KEYS
PEER_KEY_DIM = 256
PEER_TOPK = 16
PEER_BLOCK = 128

N_EVEN = (DEPTH + 1) // 2
N_ODD = DEPTH // 2

kernel_name = "hybrid_gqa_swa_mla_peer_prefix_dit"


def rmsnorm(x, g):
    xf = x.astype(jnp.float32)
    y = xf * lax.rsqrt(jnp.mean(xf * xf, axis=-1, keepdims=True) + NORM_EPS)
    return (y * g.astype(jnp.float32)).astype(x.dtype)


def modulate(h, shift, scale):
    return h * (1.0 + scale) + shift


def axial_rope_tables(rows, rot_dim):
    row = jnp.repeat(jnp.arange(rows, dtype=jnp.float32), GRID_W)
    col = jnp.tile(jnp.arange(GRID_W, dtype=jnp.float32), rows)
    half = rot_dim // 2
    inv = ROPE_THETA ** (-jnp.arange(0, half, 2, dtype=jnp.float32) / half)
    ang = jnp.concatenate([row[:, None] * inv, col[:, None] * inv], axis=-1)
    return jnp.cos(ang), jnp.sin(ang)


def apply_rope(x, cos, sin):
    xf = x.astype(jnp.float32)
    x1, x2 = xf[..., 0::2], xf[..., 1::2]
    cs, sn = cos[None, :, None, :], sin[None, :, None, :]
    out = jnp.stack([x1 * cs - x2 * sn, x1 * sn + x2 * cs], axis=-1).reshape(x.shape)
    return out.astype(x.dtype)


def attend(q, k, v, scale, mask=None, sink=None):
    B, Tq, Hq, dq = q.shape
    Hkv = k.shape[2]
    G = Hq // Hkv
    qg = q.reshape(B, Tq, Hkv, G, dq).astype(jnp.float32)
    s = jnp.einsum('bqhgd,bkhd->bhgqk', qg, k.astype(jnp.float32)) * scale
    if mask is not None:
        s = jnp.where(mask, s, NEG_INF)
    if sink is not None:
        sk = jnp.broadcast_to(sink.astype(jnp.float32).reshape(1, Hkv, G, 1, 1), s.shape[:-1] + (1,))
        s = jnp.concatenate([sk, s], axis=-1)
    p = jax.nn.softmax(s, axis=-1)
    if sink is not None:
        p = p[..., 1:]
    o = jnp.einsum('bhgqk,bkhd->bqhgd', p, v.astype(jnp.float32))
    return o.reshape(B, Tq, Hq, v.shape[-1]).astype(q.dtype)


def blocked_global(q, k, v, scale):
    B, S, Hq, dq = q.shape
    nb = S // Q_BLOCK
    qb = q.reshape(B, nb, Q_BLOCK, Hq, dq).swapaxes(0, 1)
    ob = lax.map(lambda qi: attend(qi, k, v, scale), qb)
    return ob.swapaxes(0, 1).reshape(B, S, Hq, v.shape[-1])


def blocked_window(q, k_lat, v_lat, k_ctx, v_ctx, sink, scale):
    B, S, Hq, dq = q.shape
    nb = S // Q_BLOCK
    span = Q_BLOCK + 2 * WINDOW
    n_ctx = k_ctx.shape[1]
    pad = ((0, 0), (WINDOW, WINDOW), (0, 0), (0, 0))
    kp, vp = jnp.pad(k_lat, pad), jnp.pad(v_lat, pad)
    qb = q.reshape(B, nb, Q_BLOCK, Hq, dq).swapaxes(0, 1)

    def one(args):
        i, qi = args
        start = i * Q_BLOCK
        kw = lax.dynamic_slice_in_dim(kp, start, span, axis=1)
        vw = lax.dynamic_slice_in_dim(vp, start, span, axis=1)
        q_t = start + jnp.arange(Q_BLOCK)
        k_t = start - WINDOW + jnp.arange(span)
        band = ((jnp.abs(k_t[None, :] - q_t[:, None]) <= WINDOW)
                & (k_t[None, :] >= 0) & (k_t[None, :] < S))
        mask = jnp.concatenate([jnp.ones((Q_BLOCK, n_ctx), dtype=bool), band], axis=1)
        return attend(qi, jnp.concatenate([k_ctx, kw], axis=1),
                      jnp.concatenate([v_ctx, vw], axis=1), scale, mask, sink)

    ob = lax.map(one, (jnp.arange(nb), qb))
    return ob.swapaxes(0, 1).reshape(B, S, Hq, v_lat.shape[-1])


def even_heads(h, w_in, a_qn, a_kn, b_qn, b_kn, rope):
    B, T, _ = h.shape
    offsets = np.cumsum(EVEN_WIDTHS)[:-1].tolist()
    aq, ak, av, bq, bk, bv = jnp.split(h @ w_in, offsets, axis=-1)
    heads = lambda t, n: t.reshape(B, T, n, HEAD_DIM)
    aq, ak = rmsnorm(heads(aq, A_HEADS), a_qn), rmsnorm(heads(ak, A_KV_HEADS), a_kn)
    bq, bk = rmsnorm(heads(bq, B_HEADS), b_qn), rmsnorm(heads(bk, B_KV_HEADS), b_kn)
    av, bv = heads(av, A_KV_HEADS), heads(bv, B_KV_HEADS)
    if rope is not None:
        aq, ak, bq, bk = (apply_rope(t, *rope) for t in (aq, ak, bq, bk))
    return aq, ak, av, bq, bk, bv


def mixer_even(h_ctx, h_lat, w_in, w_out, a_qn, a_kn, b_qn, b_kn, b_sink, rope, ctx_out):
    caq, cak, cav, cbq, cbk, cbv = even_heads(h_ctx, w_in, a_qn, a_kn, b_qn, b_kn, None)
    laq, lak, lav, lbq, lbk, lbv = even_heads(h_lat, w_in, a_qn, a_kn, b_qn, b_kn, rope)
    scale = HEAD_DIM ** -0.5
    B, S = h_lat.shape[:2]
    oa = blocked_global(laq, jnp.concatenate([cak, lak], axis=1),
                        jnp.concatenate([cav, lav], axis=1), scale)
    ob = blocked_window(lbq, lbk, lbv, cbk, cbv, b_sink, scale)
    o_lat = jnp.concatenate([oa.reshape(B, S, A_Q_W), ob.reshape(B, S, B_Q_W)], axis=-1) @ w_out
    o_ctx = None
    if ctx_out:
        T = h_ctx.shape[1]
        oa_c = attend(caq, cak, cav, scale)
        ob_c = attend(cbq, cbk, cbv, scale, sink=b_sink)
        o_ctx = jnp.concatenate([oa_c.reshape(B, T, A_Q_W), ob_c.reshape(B, T, B_Q_W)], axis=-1) @ w_out
    return o_ctx, o_lat


def mla_heads(h, w_in, q_a_norm, kv_a_norm, w_uq, w_ukv, qn_nope, qn_rope, kn_nope, kn_rope, rope):
    B, T, _ = h.shape
    p = h @ w_in
    cq = rmsnorm(p[..., :C_Q_LORA], q_a_norm)
    ckv = rmsnorm(p[..., C_Q_LORA:C_Q_LORA + C_KV_LORA], kv_a_norm)
    k_pe = rmsnorm(p[..., C_Q_LORA + C_KV_LORA:].reshape(B, T, 1, C_ROPE), kn_rope)
    q = (cq @ w_uq).reshape(B, T, C_HEADS, C_NOPE + C_ROPE)
    kv = (ckv @ w_ukv).reshape(B, T, C_HEADS, C_NOPE + C_V)
    q_nope, q_pe = rmsnorm(q[..., :C_NOPE], qn_nope), rmsnorm(q[..., C_NOPE:], qn_rope)
    k_nope, v = rmsnorm(kv[..., :C_NOPE], kn_nope), kv[..., C_NOPE:]
    if rope is not None:
        q_pe, k_pe = apply_rope(q_pe, *rope), apply_rope(k_pe, *rope)
    q = jnp.concatenate([q_nope, q_pe], axis=-1)
    k = jnp.concatenate([k_nope, jnp.broadcast_to(k_pe, (B, T, C_HEADS, C_ROPE))], axis=-1)
    return q, k, v


def mixer_mla(h_ctx, h_lat, w_in, q_a_norm, kv_a_norm, w_uq, w_ukv,
              qn_nope, qn_rope, kn_nope, kn_rope, w_out, rope, ctx_out):
    args = (w_in, q_a_norm, kv_a_norm, w_uq, w_ukv, qn_nope, qn_rope, kn_nope, kn_rope)
    cq, ck, cv = mla_heads(h_ctx, *args, None)
    lq, lk, lv = mla_heads(h_lat, *args, rope)
    scale = (C_NOPE + C_ROPE) ** -0.5
    B, S = h_lat.shape[:2]
    o_lat = blocked_global(lq, jnp.concatenate([ck, lk], axis=1),
                           jnp.concatenate([cv, lv], axis=1), scale)
    o_lat = o_lat.reshape(B, S, C_HEADS * C_V) @ w_out
    o_ctx = None
    if ctx_out:
        o_ctx = attend(cq, ck, cv, scale).reshape(B, h_ctx.shape[1], C_HEADS * C_V) @ w_out
    return o_ctx, o_lat


def peer_ffn(h, w_query, sub_keys, expert_u, expert_v):
    B, T, D = h.shape
    tokens = h.reshape(-1, PEER_BLOCK, D)

    def block(hb):
        q = (hb @ w_query).reshape(PEER_BLOCK, PEER_HEADS, 2, PEER_KEY_DIM // 2)
        s = jnp.einsum('thpk,hpnk->thpn', q.astype(jnp.float32), sub_keys.astype(jnp.float32))
        s_top, i_top = lax.top_k(s, PEER_TOPK)
        cand = (s_top[:, :, 0, :, None] + s_top[:, :, 1, None, :]).reshape(PEER_BLOCK, PEER_HEADS, -1)
        cand_idx = (i_top[:, :, 0, :, None] * PEER_KEYS + i_top[:, :, 1, None, :]).reshape(PEER_BLOCK, PEER_HEADS, -1)
        best, pos = lax.top_k(cand, PEER_TOPK)
        expert = jnp.take_along_axis(cand_idx, pos, axis=-1)
        g = jax.nn.softmax(best, axis=-1)
        u = expert_u[expert]
        v = expert_v[expert]
        a = jax.nn.gelu(jnp.einsum('td,thkd->thk', hb, u).astype(jnp.float32), approximate=False)
        return jnp.einsum('thk,thkd->td', (g * a).astype(hb.dtype), v)

    return lax.map(block, tokens).reshape(B, T, D)


def setup_inputs(seed: int = 0) -> dict:
    key = jax.random.key(seed)
    ks = iter(jax.random.split(key, 40))
    nrm = lambda shape, s: jax.random.normal(next(ks), shape, jnp.float32) * s
    gain = lambda shape: 1.0 + nrm(shape, 0.05)
    D = D_MODEL
    return {
        "x": nrm((BATCH, SEQ, D), 1.0),
        "c": nrm((BATCH, D), 1.0),
        "ctx": nrm((BATCH, CTX_LEN, D), 1.0),
        "c_ctx": nrm((D,), 1.0),
        "mod_w": nrm((DEPTH, D, 6 * D), D ** -0.5),
        "mod_b": nrm((DEPTH, 6 * D), 0.02),
        "mix_norm": gain((DEPTH, D)),
        "ffn_norm": gain((DEPTH, D)),
        "even_w_in": nrm((N_EVEN, D, EVEN_IN_W), D ** -0.5),
        "even_w_out": nrm((N_EVEN, EVEN_OUT_W, D), EVEN_OUT_W ** -0.5),
        "a_q_norm": gain((N_EVEN, HEAD_DIM)),
        "a_k_norm": gain((N_EVEN, HEAD_DIM)),
        "b_q_norm": gain((N_EVEN, HEAD_DIM)),
        "b_k_norm": gain((N_EVEN, HEAD_DIM)),
        "b_sink": nrm((N_EVEN, B_HEADS), 0.5),
        "mla_w_in": nrm((N_ODD, D, C_IN_W), D ** -0.5),
        "mla_q_a_norm": gain((N_ODD, C_Q_LORA)),
        "mla_kv_a_norm": gain((N_ODD, C_KV_LORA)),
        "mla_w_uq": nrm((N_ODD, C_Q_LORA, C_HEADS * (C_NOPE + C_ROPE)), C_Q_LORA ** -0.5),
        "mla_w_ukv": nrm((N_ODD, C_KV_LORA, C_HEADS * (C_NOPE + C_V)), C_KV_LORA ** -0.5),
        "mla_q_nope_norm": gain((N_ODD, C_NOPE)),
        "mla_q_rope_norm": gain((N_ODD, C_ROPE)),
        "mla_k_nope_norm": gain((N_ODD, C_NOPE)),
        "mla_k_rope_norm": gain((N_ODD, C_ROPE)),
        "mla_w_out": nrm((N_ODD, C_HEADS * C_V, D), (C_HEADS * C_V) ** -0.5),
        "peer_w_query": nrm((DEPTH, D, PEER_HEADS * PEER_KEY_DIM), D ** -0.5),
        "peer_sub_keys": nrm((DEPTH, PEER_HEADS, 2, PEER_KEYS, PEER_KEY_DIM // 2), (PEER_KEY_DIM // 2) ** -0.5),
        "peer_u": nrm((DEPTH, PEER_EXPERTS, D), D ** -0.5),
        "peer_v": nrm((DEPTH, PEER_EXPERTS, D), PEER_HEADS ** -0.5),
    }


def reference(x, c, ctx, c_ctx, mod_w, mod_b, mix_norm, ffn_norm,
              even_w_in, even_w_out, a_q_norm, a_k_norm, b_q_norm, b_k_norm, b_sink,
              mla_w_in, mla_q_a_norm, mla_kv_a_norm, mla_w_uq, mla_w_ukv,
              mla_q_nope_norm, mla_q_rope_norm, mla_k_nope_norm, mla_k_rope_norm, mla_w_out,
              peer_w_query, peer_sub_keys, peer_u, peer_v):
    n_lat = x.shape[1]
    rows = n_lat // GRID_W
    rope_attn = axial_rope_tables(rows, HEAD_DIM)
    rope_mla = axial_rope_tables(rows, C_ROPE)
    for l in range(DEPTH):
        ctx_out = l < DEPTH - 1
        lat_mod = [m[:, None, :] for m in jnp.split(jax.nn.silu(c) @ mod_w[l] + mod_b[l], 6, axis=-1)]
        ctx_mod = jnp.split(jax.nn.silu(c_ctx) @ mod_w[l] + mod_b[l], 6, axis=-1)
        h_lat = modulate(rmsnorm(x, mix_norm[l]), lat_mod[0], lat_mod[1])
        h_ctx = modulate(rmsnorm(ctx, mix_norm[l]), ctx_mod[0], ctx_mod[1])
        if l % 2 == 0:
            e = l // 2
            o_ctx, o_lat = mixer_even(h_ctx, h_lat, even_w_in[e], even_w_out[e], a_q_norm[e], a_k_norm[e],
                                      b_q_norm[e], b_k_norm[e], b_sink[e], rope_attn, ctx_out)
        else:
            o = l // 2
            o_ctx, o_lat = mixer_mla(h_ctx, h_lat, mla_w_in[o], mla_q_a_norm[o], mla_kv_a_norm[o],
                                     mla_w_uq[o], mla_w_ukv[o], mla_q_nope_norm[o], mla_q_rope_norm[o],
                                     mla_k_nope_norm[o], mla_k_rope_norm[o], mla_w_out[o], rope_mla, ctx_out)
        x = x + lat_mod[2] * o_lat
        x = x + lat_mod[5] * peer_ffn(modulate(rmsnorm(x, ffn_norm[l]), lat_mod[3], lat_mod[4]),
                                      peer_w_query[l], peer_sub_keys[l], peer_u[l], peer_v[l])
        if ctx_out:
            ctx = ctx + ctx_mod[2] * o_ctx
            ctx = ctx + ctx_mod[5] * peer_ffn(modulate(rmsnorm(ctx, ffn_norm[l]), ctx_mod[3], ctx_mod[4]),
                                              peer_w_query[l], peer_sub_keys[l], peer_u[l], peer_v[l])
    return x
```

```python
import functools
import math

import numpy as np
import jax
import jax.numpy as jnp
from jax import lax
from jax.experimental import pallas as pl
from jax.experimental.pallas import tpu as pltpu

F32 = jnp.float32
BF16 = jnp.bfloat16

GRID_W = 64
ROPE_THETA = 10000.0
NORM_EPS = 1e-6
NEG = -1e30
HEAD_DIM = 64
A_HEADS, A_KV_HEADS = 8, 2
B_HEADS, B_KV_HEADS = 8, 2
WINDOW = 128
C_HEADS, C_Q_LORA, C_KV_LORA = 8, 384, 256
C_NOPE, C_ROPE, C_V = 128, 64, 128
PEER_HEADS, PEER_KEYS, PEER_TOPK = 8, 128, 16
PEER_EXPERTS = PEER_KEYS * PEER_KEYS

LANES = 128
SUBLANES = 8
VMEM_LIMIT_BYTES = 56 * 1024 * 1024

TOK_BLOCK = 256
PEER_TOK_BLOCK = 128
ATTN_TQ_GQA = 256
ATTN_TQ_MLA = 512
ATTN_TK = 768
WIN_BLOCK = 128

_CAND_COUNTS = tuple(PEER_TOPK // (a + 1) for a in range(PEER_TOPK))
_ROWS_PER_EXPERT = 4


def _params(semantics):
    return pltpu.CompilerParams(dimension_semantics=semantics, vmem_limit_bytes=VMEM_LIMIT_BYTES)


def _rms(x, gain):
    ms = jnp.mean(x * x, axis=-1, keepdims=True)
    return x * lax.rsqrt(ms + NORM_EPS) * gain


def _lane_iota(shape):
    return lax.broadcasted_iota(jnp.int32, shape, len(shape) - 1)


def _head_norm_rope(y, gain, cos, sin):
    lane = _lane_iota(y.shape)
    first = lane < HEAD_DIM
    y2 = y * y
    s_lo = jnp.sum(jnp.where(first, y2, 0.0), axis=-1, keepdims=True)
    s_hi = jnp.sum(jnp.where(first, 0.0, y2), axis=-1, keepdims=True)
    ms = jnp.where(first, s_lo, s_hi) * (1.0 / HEAD_DIM)
    yn = y * lax.rsqrt(ms + NORM_EPS) * gain
    half = HEAD_DIM // 2
    partner = jnp.where((lane % HEAD_DIM) < half,
                        pltpu.roll(yn, LANES - half, 1), pltpu.roll(yn, half, 1))
    return yn * cos + partner * sin


def _mods_kernel(c_ref, w_ref, b_ref, o_ref):
    c = c_ref[...]
    s = c * jax.nn.sigmoid(c)
    o_ref[0] = jnp.dot(s.astype(BF16), w_ref[0].astype(BF16), preferred_element_type=F32) + b_ref[0]


def _mods(cc, mod_w, mod_b):
    depth, d, n = mod_w.shape
    tn = 1536
    return pl.pallas_call(
        _mods_kernel,
        grid=(depth, n // tn),
        in_specs=[pl.BlockSpec((SUBLANES, d), lambda l, j: (0, 0)),
                  pl.BlockSpec((1, d, tn), lambda l, j: (l, 0, j)),
                  pl.BlockSpec((1, 1, tn), lambda l, j: (l, 0, j))],
        out_specs=pl.BlockSpec((1, SUBLANES, tn), lambda l, j: (l, 0, j)),
        out_shape=jax.ShapeDtypeStruct((depth, SUBLANES, n), F32),
        compiler_params=_params(("arbitrary", "arbitrary")),
        name="adaln_mods",
    )(cc, mod_w, mod_b.reshape(depth, 1, n))


def _mod_spec(d, n_ctx_blocks):
    return pl.BlockSpec((1, 1, 1, d), lambda b, t, *_: (b, (t >= n_ctx_blocks).astype(jnp.int32), 0, 0))


def _even_in_kernel(x_ref, shift_ref, scale_ref, g_ref, w_ref, gains_ref, cos_ref, sin_ref,
                    aq_ref, ak_ref, av_ref, bq_ref, bk_ref, bv_ref, *, q_scale):
    hn = _rms(x_ref[0], g_ref[...]) * (1.0 + scale_ref[0, 0]) + shift_ref[0, 0]
    y = jnp.dot(hn.astype(BF16), w_ref[...], preferred_element_type=F32)
    cos, sin = cos_ref[...], sin_ref[...]

    def heads(col0, n_heads, gain_row, out_ref, scale):
        for j in range(n_heads // 2):
            c = col0 + j * LANES
            o = _head_norm_rope(y[:, c:c + LANES], gains_ref[gain_row:gain_row + 1, :], cos, sin) * scale
            out_ref[0, 2 * j] = o[:, :HEAD_DIM].astype(out_ref.dtype)
            out_ref[0, 2 * j + 1] = o[:, HEAD_DIM:].astype(out_ref.dtype)

    def values(col0, n_heads, out_ref):
        for h in range(n_heads):
            c = col0 + h * HEAD_DIM
            out_ref[0, h] = y[:, c:c + HEAD_DIM].astype(out_ref.dtype)

    a_q, a_kv = A_HEADS * HEAD_DIM, A_KV_HEADS * HEAD_DIM
    b_q, b_kv = B_HEADS * HEAD_DIM, B_KV_HEADS * HEAD_DIM
    c0 = 0
    heads(c0, A_HEADS, 0, aq_ref, q_scale); c0 += a_q
    heads(c0, A_KV_HEADS, 1, ak_ref, 1.0); c0 += a_kv
    values(c0, A_KV_HEADS, av_ref); c0 += a_kv
    heads(c0, B_HEADS, 2, bq_ref, q_scale); c0 += b_q
    heads(c0, B_KV_HEADS, 3, bk_ref, 1.0); c0 += b_kv
    values(c0, B_KV_HEADS, bv_ref)


def _even_in(xc, shift, scale, norm_g, w_in, gains, cos, sin, n_ctx, tb):
    b, t, d = xc.shape
    n_ctx_blocks = n_ctx // tb
    tok = lambda bi, ti: (bi, ti, 0)
    head_spec = lambda n: pl.BlockSpec((1, n, tb, HEAD_DIM), lambda bi, ti: (bi, 0, ti, 0))
    head_shape = lambda n: jax.ShapeDtypeStruct((b, n, t, HEAD_DIM), BF16)
    return pl.pallas_call(
        functools.partial(_even_in_kernel, q_scale=HEAD_DIM ** -0.5),
        grid=(b, t // tb),
        in_specs=[pl.BlockSpec((1, tb, d), tok),
                  _mod_spec(d, n_ctx_blocks), _mod_spec(d, n_ctx_blocks),
                  pl.BlockSpec((1, d), lambda bi, ti: (0, 0)),
                  pl.BlockSpec(w_in.shape, lambda bi, ti: (0, 0)),
                  pl.BlockSpec(gains.shape, lambda bi, ti: (0, 0)),
                  pl.BlockSpec((tb, LANES), lambda bi, ti: (ti, 0)),
                  pl.BlockSpec((tb, LANES), lambda bi, ti: (ti, 0))],
        out_specs=[head_spec(A_HEADS), head_spec(A_KV_HEADS), head_spec(A_KV_HEADS),
                   head_spec(B_HEADS), head_spec(B_KV_HEADS), head_spec(B_KV_HEADS)],
        out_shape=[head_shape(A_HEADS), head_shape(A_KV_HEADS), head_shape(A_KV_HEADS),
                   head_shape(B_HEADS), head_shape(B_KV_HEADS), head_shape(B_KV_HEADS)],
        compiler_params=_params(("parallel", "parallel")),
        name="even_in_proj",
    )(xc, shift, scale, norm_g, w_in, gains, cos, sin)


def _mla_in_kernel(x_ref, shift_ref, scale_ref, g_ref, w_in_ref, qa_g_ref, kva_g_ref, w_uq_ref, w_ukv_ref,
                   gains_ref, cos_ref, sin_ref, q_ref, k_ref, v_ref, *, q_scale):
    hn = _rms(x_ref[0], g_ref[...]) * (1.0 + scale_ref[0, 0]) + shift_ref[0, 0]
    p = jnp.dot(hn.astype(BF16), w_in_ref[...], preferred_element_type=F32)
    cq = _rms(p[:, :C_Q_LORA], qa_g_ref[...])
    ckv = _rms(p[:, C_Q_LORA:C_Q_LORA + C_KV_LORA], kva_g_ref[...])
    cos, sin = cos_ref[...], sin_ref[...]
    lane = _lane_iota(cos.shape)
    first = lane < C_ROPE
    q_nope_g, q_rope_g = gains_ref[0:1, :], gains_ref[1:2, :]
    k_nope_g, k_rope_g = gains_ref[2:3, :], gains_ref[3:4, :]

    c_pe = C_Q_LORA + C_KV_LORA
    k_pe = _head_norm_rope(p[:, c_pe:c_pe + LANES], k_rope_g, cos, sin)
    k_pe = jnp.where(first, k_pe, 0.0).astype(k_ref.dtype)

    q = jnp.dot(cq.astype(BF16), w_uq_ref[...], preferred_element_type=F32)
    kv = jnp.dot(ckv.astype(BF16), w_ukv_ref[...], preferred_element_type=F32)
    rope0 = C_HEADS * C_NOPE
    for j in range(C_HEADS // 2):
        r = _head_norm_rope(q[:, rope0 + j * LANES:rope0 + (j + 1) * LANES], q_rope_g, cos, sin) * q_scale
        for h, rr in ((2 * j, r), (2 * j + 1, pltpu.roll(r, C_ROPE, 1))):
            qn = _rms(q[:, h * C_NOPE:(h + 1) * C_NOPE], q_nope_g) * q_scale
            q_ref[0, h, :, :C_NOPE] = qn.astype(q_ref.dtype)
            q_ref[0, h, :, C_NOPE:] = jnp.where(first, rr, 0.0).astype(q_ref.dtype)
    for h in range(C_HEADS):
        c = h * (C_NOPE + C_V)
        k_ref[0, h, :, :C_NOPE] = _rms(kv[:, c:c + C_NOPE], k_nope_g).astype(k_ref.dtype)
        k_ref[0, h, :, C_NOPE:] = k_pe
        v_ref[0, h] = kv[:, c + C_NOPE:c + C_NOPE + C_V].astype(v_ref.dtype)


def _mla_in(xc, shift, scale, norm_g, w_in, qa_g, kva_g, w_uq, w_ukv, gains, cos, sin, n_ctx, tb):
    b, t, d = xc.shape
    n_ctx_blocks = n_ctx // tb
    full = lambda a: pl.BlockSpec(a.shape, lambda bi, ti: (0,) * a.ndim)
    dqk = C_NOPE + LANES
    hspec = lambda w: pl.BlockSpec((1, C_HEADS, tb, w), lambda bi, ti: (bi, 0, ti, 0))
    hshape = lambda w: jax.ShapeDtypeStruct((b, C_HEADS, t, w), BF16)
    return pl.pallas_call(
        functools.partial(_mla_in_kernel, q_scale=(C_NOPE + C_ROPE) ** -0.5),
        grid=(b, t // tb),
        in_specs=[pl.BlockSpec((1, tb, d), lambda bi, ti: (bi, ti, 0)),
                  _mod_spec(d, n_ctx_blocks), _mod_spec(d, n_ctx_blocks),
                  full(norm_g), full(w_in), full(qa_g), full(kva_g), full(w_uq), full(w_ukv), full(gains),
                  pl.BlockSpec((tb, LANES), lambda bi, ti: (ti, 0)),
                  pl.BlockSpec((tb, LANES), lambda bi, ti: (ti, 0))],
        out_specs=[hspec(dqk), hspec(dqk), hspec(C_V)],
        out_shape=[hshape(dqk), hshape(dqk), hshape(C_V)],
        compiler_params=_params(("parallel", "parallel")),
        name="mla_in_proj",
    )(xc, shift, scale, norm_g, w_in, qa_g, kva_g, w_uq, w_ukv, gains, cos, sin)


def _softmax_step(s, v, m_sc, l_sc, acc_sc):
    m_prev = m_sc[...]
    m_new = jnp.maximum(m_prev, jnp.max(s, axis=-1, keepdims=True))
    alpha = jnp.exp(m_prev - m_new)
    p = jnp.exp(s - m_new)
    l_sc[...] = alpha * l_sc[...] + jnp.sum(p, axis=-1, keepdims=True)
    acc_sc[...] = alpha * acc_sc[...] + jnp.dot(p.astype(v.dtype), v, preferred_element_type=F32)
    m_sc[...] = m_new


def _scores(q_ref, k_ref):
    g, tq, dq = q_ref.shape[1:]
    q = q_ref[0].reshape(g * tq, dq)
    return lax.dot_general(q, k_ref[0, 0], (((1,), (1,)), ((), ())), preferred_element_type=F32)


def _write_heads(o_ref, l_sc, acc_sc, g, tq, dv):
    out = acc_sc[...] / l_sc[...]
    for h in range(g):
        o_ref[0, :, h * dv:(h + 1) * dv] = out[h * tq:(h + 1) * tq].astype(o_ref.dtype)


def _global_attn_kernel(q_ref, k_ref, v_ref, o_ref, m_sc, l_sc, acc_sc, *, n_ctx, tq, tk):
    qi, ki = pl.program_id(2), pl.program_id(3)
    g, dv = q_ref.shape[1], v_ref.shape[-1]

    @pl.when(ki == 0)
    def _():
        m_sc[...] = jnp.full(m_sc.shape, NEG, F32)
        l_sc[...] = jnp.zeros(l_sc.shape, F32)
        acc_sc[...] = jnp.zeros(acc_sc.shape, F32)

    ctx_q = qi * tq < n_ctx

    @pl.when(jnp.logical_not(ctx_q))
    def _():
        _softmax_step(_scores(q_ref, k_ref), v_ref[0, 0], m_sc, l_sc, acc_sc)

    @pl.when(jnp.logical_and(ctx_q, ki * tk < n_ctx))
    def _():
        s = _scores(q_ref, k_ref)
        kpos = ki * tk + _lane_iota(s.shape)
        _softmax_step(jnp.where(kpos < n_ctx, s, NEG), v_ref[0, 0], m_sc, l_sc, acc_sc)

    @pl.when(ki == pl.num_programs(3) - 1)
    def _():
        _write_heads(o_ref, l_sc, acc_sc, g, tq, dv)


def _global_attn(q, k, v, n_ctx, tq, tk):
    b, hq, t, dq = q.shape
    hkv, dv = k.shape[1], v.shape[-1]
    g = hq // hkv
    tq = min(tq, n_ctx)
    tk = max(c for c in range(LANES, tk + 1, LANES) if t % c == 0)
    assert n_ctx % tq == 0 and t % tq == 0
    return pl.pallas_call(
        functools.partial(_global_attn_kernel, n_ctx=n_ctx, tq=tq, tk=tk),
        grid=(b, hkv, t // tq, t // tk),
        in_specs=[pl.BlockSpec((1, g, tq, dq), lambda bi, hi, qi, ki: (bi, hi, qi, 0)),
                  pl.BlockSpec((1, 1, tk, dq), lambda bi, hi, qi, ki: (bi, hi, ki, 0)),
                  pl.BlockSpec((1, 1, tk, dv), lambda bi, hi, qi, ki: (bi, hi, ki, 0))],
        out_specs=pl.BlockSpec((1, tq, g * dv), lambda bi, hi, qi, ki: (bi, qi, hi)),
        out_shape=jax.ShapeDtypeStruct((b, t, hq * dv), BF16),
        scratch_shapes=[pltpu.VMEM((g * tq, 1), F32), pltpu.VMEM((g * tq, 1), F32),
                        pltpu.VMEM((g * tq, dv), F32)],
        compiler_params=_params(("parallel", "parallel", "parallel", "arbitrary")),
        name="global_attention",
    )(q, k, v)


def _window_attn_kernel(sink_ref, q_ref, k_ref, v_ref, o_ref, m_sc, l_sc, acc_sc, *, n_ctx, tq, n_blocks):
    hi, qi, j = pl.program_id(1), pl.program_id(2), pl.program_id(3)
    g, dv = q_ref.shape[1], v_ref.shape[-1]
    n_ctx_blocks = n_ctx // WIN_BLOCK
    q_blocks = tq // WIN_BLOCK

    @pl.when(j == 0)
    def _():
        for h in range(g):
            m_sc[h * tq:(h + 1) * tq, :] = jnp.full((tq, 1), sink_ref[hi * g + h], F32)
        l_sc[...] = jnp.ones(l_sc.shape, F32)
        acc_sc[...] = jnp.zeros(acc_sc.shape, F32)

    @pl.when(j < n_ctx_blocks)
    def _():
        _softmax_step(_scores(q_ref, k_ref), v_ref[0, 0], m_sc, l_sc, acc_sc)

    kb = qi * q_blocks - 1 + (j - n_ctx_blocks)
    lat_q = qi * tq >= n_ctx
    in_range = jnp.logical_and(kb >= n_ctx_blocks, kb < n_blocks)

    @pl.when(jnp.logical_and(j >= n_ctx_blocks, jnp.logical_and(lat_q, in_range)))
    def _():
        s = _scores(q_ref, k_ref)
        row = lax.broadcasted_iota(jnp.int32, s.shape, 0)
        qpos = qi * tq + row % tq
        kpos = kb * WIN_BLOCK + _lane_iota(s.shape)
        _softmax_step(jnp.where(jnp.abs(kpos - qpos) <= WINDOW, s, NEG), v_ref[0, 0], m_sc, l_sc, acc_sc)

    @pl.when(j == pl.num_programs(3) - 1)
    def _():
        _write_heads(o_ref, l_sc, acc_sc, g, tq, dv)


def _window_attn(q, k, v, sink, n_ctx, tq):
    b, hq, t, dq = q.shape
    hkv, dv = k.shape[1], v.shape[-1]
    g = hq // hkv
    tq = min(tq, n_ctx)
    assert n_ctx % tq == 0 and t % tq == 0 and tq % WIN_BLOCK == 0 and n_ctx % WIN_BLOCK == 0
    n_blocks = t // WIN_BLOCK
    n_ctx_blocks = n_ctx // WIN_BLOCK
    q_blocks = tq // WIN_BLOCK
    n_steps = n_ctx_blocks + q_blocks + 2

    def kv_index(bi, hi, qi, j):
        kb = jnp.where(j < n_ctx_blocks, j, qi * q_blocks - 1 + (j - n_ctx_blocks))
        return (bi, hi, jnp.clip(kb, 0, n_blocks - 1), 0)

    return pl.pallas_call(
        functools.partial(_window_attn_kernel, n_ctx=n_ctx, tq=tq, n_blocks=n_blocks),
        grid=(b, hkv, t // tq, n_steps),
        in_specs=[pl.BlockSpec(memory_space=pltpu.SMEM),
                  pl.BlockSpec((1, g, tq, dq), lambda bi, hi, qi, j: (bi, hi, qi, 0)),
                  pl.BlockSpec((1, 1, WIN_BLOCK, dq), kv_index),
                  pl.BlockSpec((1, 1, WIN_BLOCK, dv), kv_index)],
        out_specs=pl.BlockSpec((1, tq, g * dv), lambda bi, hi, qi, j: (bi, qi, hi)),
        out_shape=jax.ShapeDtypeStruct((b, t, hq * dv), BF16),
        scratch_shapes=[pltpu.VMEM((g * tq, 1), F32), pltpu.VMEM((g * tq, 1), F32),
                        pltpu.VMEM((g * tq, dv), F32)],
        compiler_params=_params(("parallel", "parallel", "parallel", "arbitrary")),
        name="window_attention",
    )(sink, q, k, v)


def _out_proj_kernel(*refs, n_in):
    x_ref, gate_ref = refs[0], refs[1]
    o_refs, w_refs = refs[2:2 + n_in], refs[2 + n_in:2 + 2 * n_in]
    out_ref = refs[2 + 2 * n_in]
    acc = jnp.dot(o_refs[0][0], w_refs[0][...], preferred_element_type=F32)
    for o_ref, w_ref in zip(o_refs[1:], w_refs[1:]):
        acc += jnp.dot(o_ref[0], w_ref[...], preferred_element_type=F32)
    out_ref[0] = x_ref[0] + gate_ref[0, 0] * acc


def _out_proj(xc, gate, outs, weights, n_ctx, tb):
    b, t, d = xc.shape
    n_in = len(outs)
    tok = lambda bi, ti: (bi, ti, 0)
    return pl.pallas_call(
        functools.partial(_out_proj_kernel, n_in=n_in),
        grid=(b, t // tb),
        in_specs=[pl.BlockSpec((1, tb, d), tok), _mod_spec(d, n_ctx // tb)]
        + [pl.BlockSpec((1, tb, o.shape[-1]), tok) for o in outs]
        + [pl.BlockSpec(w.shape, lambda bi, ti: (0, 0)) for w in weights],
        out_specs=pl.BlockSpec((1, tb, d), tok),
        out_shape=jax.ShapeDtypeStruct((b, t, d), F32),
        compiler_params=_params(("parallel", "parallel")),
        name="out_proj_residual",
    )(xc, gate, *outs, *weights)


def _top_rows(s, k):
    n, width = s.shape
    rows = lax.broadcasted_iota(jnp.int32, s.shape, 0)
    slot = lax.broadcasted_iota(jnp.int32, (k, width), 0)
    vals = jnp.zeros((k, width), F32)
    ids = jnp.zeros((k, width), jnp.int32)
    for r in range(k):
        m = jnp.max(s, axis=0, keepdims=True)
        i = jnp.min(jnp.where(s == m, rows, n), axis=0, keepdims=True)
        vals = jnp.where(slot == r, m, vals)
        ids = jnp.where(slot == r, i, ids)
        s = jnp.where(rows == i, -jnp.inf, s)
    return vals, ids


def _route_kernel(x_ref, shift_ref, scale_ref, g_ref, wq_ref, keys_ref, hn_ref, ids_ref, gate_ref, hn_sc):
    @pl.when(pl.program_id(2) == 0)
    def _():
        hn = _rms(x_ref[0], g_ref[...]) * (1.0 + scale_ref[0, 0]) + shift_ref[0, 0]
        hn_ref[0] = hn
        hn_sc[...] = hn.astype(BF16)

    q = jnp.dot(hn_sc[...], wq_ref[...], preferred_element_type=F32)
    tops = []
    for p in range(2):
        qp = q[:, p * PEER_KEYS:(p + 1) * PEER_KEYS].astype(BF16)
        s_t = lax.dot_general(keys_ref[0, p], qp, (((1,), (1,)), ((), ())),
                              preferred_element_type=F32)
        tops.append(_top_rows(s_t, PEER_TOPK))
    (s0, i0), (s1, i1) = tops
    cand, cand_e = [], []
    sub = lax.broadcasted_iota(jnp.int32, (SUBLANES, s0.shape[1]), 0)
    for a, nb in enumerate(_CAND_COUNTS):
        if nb == 1:
            continue
        for b0 in range(0, nb, SUBLANES):
            live = sub < nb - b0
            cand.append(jnp.where(live, s0[a:a + 1] + s1[b0:b0 + SUBLANES], -jnp.inf))
            cand_e.append(i0[a:a + 1] * PEER_KEYS + i1[b0:b0 + SUBLANES])
    single = _CAND_COUNTS.index(1)
    cand.append(s0[single:] + s1[0:1])
    cand_e.append(i0[single:] * PEER_KEYS + i1[0:1])
    cand = jnp.concatenate(cand, axis=0)
    cand_e = jnp.concatenate(cand_e, axis=0)
    best, pos = _top_rows(cand, PEER_TOPK)
    rows = lax.broadcasted_iota(jnp.int32, cand.shape, 0)
    slot = lax.broadcasted_iota(jnp.int32, best.shape, 0)
    expert = jnp.zeros(best.shape, jnp.int32)
    for r in range(PEER_TOPK):
        picked = jnp.sum(jnp.where(rows == pos[r:r + 1], cand_e, 0), axis=0, keepdims=True)
        expert = jnp.where(slot == r, picked, expert)
    e = jnp.exp(best - best[0:1])
    gate_ref[0] = e / jnp.sum(e, axis=0, keepdims=True)
    ids_ref[0] = expert * _ROWS_PER_EXPERT


def _route(xc, shift, scale, norm_g, w_query, sub_keys, n_ctx, tb):
    b, t, d = xc.shape
    nk = PEER_HEADS * PEER_TOPK
    return pl.pallas_call(
        _route_kernel,
        grid=(b, t // tb, PEER_HEADS),
        in_specs=[pl.BlockSpec((1, tb, d), lambda bi, ti, h: (bi, ti, 0)),
                  _mod_spec(d, n_ctx // tb), _mod_spec(d, n_ctx // tb),
                  pl.BlockSpec((1, d), lambda bi, ti, h: (0, 0)),
                  pl.BlockSpec((d, 2 * PEER_KEYS), lambda bi, ti, h: (0, h)),
                  pl.BlockSpec((1, 2, PEER_KEYS, PEER_KEYS), lambda bi, ti, h: (h, 0, 0, 0))],
        out_specs=[pl.BlockSpec((1, tb, d), lambda bi, ti, h: (bi, ti, 0)),
                   pl.BlockSpec((1, PEER_TOPK, tb), lambda bi, ti, h: (bi, h, ti)),
                   pl.BlockSpec((1, PEER_TOPK, tb), lambda bi, ti, h: (bi, h, ti))],
        out_shape=[jax.ShapeDtypeStruct((b, t, d), F32),
                   jax.ShapeDtypeStruct((b, nk, t), jnp.int32),
                   jax.ShapeDtypeStruct((b, nk, t), F32)],
        scratch_shapes=[pltpu.VMEM((tb, d), BF16)],
        compiler_params=_params(("parallel", "parallel", "arbitrary")),
        name="peer_route",
    )(xc, shift, scale, norm_g, w_query, sub_keys)


def _gather_tiles(ids_ref, t, table_ref, tiles_ref):
    for k in range(ids_ref.shape[1]):
        r = pl.multiple_of(ids_ref[0, k, t], _ROWS_PER_EXPERT)
        tiles_ref[k * _ROWS_PER_EXPERT:(k + 1) * _ROWS_PER_EXPERT, :] = table_ref[pl.ds(r, _ROWS_PER_EXPERT), :]
    return pltpu.bitcast(tiles_ref[...], BF16)


def _split_bf16(x):
    hi = x.astype(BF16)
    return hi, (x - hi.astype(F32)).astype(BF16)


def _stack_split_bf16(x):
    hi = x.astype(BF16).astype(F32)
    return jnp.concatenate([hi, x - hi], axis=0).astype(BF16)


def _diag_mask(shape):
    row = lax.broadcasted_iota(jnp.int32, shape, 0)
    return (row % SUBLANES) == (_lane_iota(shape) % SUBLANES)


def _peer_u_kernel(ids_ref, hn_ref, gate_ref, table_ref, group_ref, w_ref, tiles_sc, dots_sc):
    tb = hn_ref.shape[1]

    def token(t, carry):
        tiles = _gather_tiles(ids_ref, t, table_ref, tiles_sc)
        xt = _stack_split_bf16(hn_ref[0, t])
        s = lax.dot_general(xt, tiles, (((1,), (1,)), ((), ())), preferred_element_type=F32)
        s = s[:SUBLANES] + s[SUBLANES:]
        dots_sc[pl.ds(t, 1), :] = jnp.sum(jnp.where(_diag_mask(s.shape), s, 0.0), axis=0, keepdims=True)
        return carry

    lax.fori_loop(0, tb, token, 0)
    hi, lo = _split_bf16(dots_sc[...])
    a = (jnp.dot(hi, group_ref[...], preferred_element_type=F32)
         + jnp.dot(lo, group_ref[...], preferred_element_type=F32))
    act = 0.5 * a * (1.0 + lax.erf(a * (2.0 ** -0.5)))
    w_ref[0] = act * gate_ref[0].T


def _peer_u(ids, hn, gates, table, group, tb):
    b, nk, t = ids.shape
    d = hn.shape[-1]
    chunks = d // LANES
    return pl.pallas_call(
        _peer_u_kernel,
        grid=(b, t // tb),
        in_specs=[pl.BlockSpec((1, nk, tb), lambda bi, ti: (bi, 0, ti), memory_space=pltpu.SMEM),
                  pl.BlockSpec((1, tb, chunks, LANES), lambda bi, ti: (bi, ti, 0, 0)),
                  pl.BlockSpec((1, nk, tb), lambda bi, ti: (bi, 0, ti)),
                  pl.BlockSpec(memory_space=pltpu.VMEM),
                  pl.BlockSpec(group.shape, lambda bi, ti: (0, 0))],
        out_specs=pl.BlockSpec((1, tb, nk), lambda bi, ti: (bi, ti, 0)),
        out_shape=jax.ShapeDtypeStruct((b, t, nk), F32),
        scratch_shapes=[pltpu.VMEM((nk * _ROWS_PER_EXPERT, LANES), jnp.uint32),
                        pltpu.VMEM((tb, nk * SUBLANES), F32)],
        compiler_params=_params(("arbitrary", "arbitrary")),
        name="peer_expert_in",
    )(ids, hn.reshape(b, t, chunks, LANES), gates, table, group)


def _peer_v_kernel(ids_ref, w_ref, x_ref, gate_ref, table_ref, expand_ref, out_ref, tiles_sc, wrep_sc):
    tb = w_ref.shape[1]
    hi, lo = _split_bf16(w_ref[0])
    wrep_sc[...] = (jnp.dot(hi, expand_ref[...], preferred_element_type=F32)
                    + jnp.dot(lo, expand_ref[...], preferred_element_type=F32))
    gate = gate_ref[0, 0]

    def token(t, carry):
        tiles = _gather_tiles(ids_ref, t, table_ref, tiles_sc)
        w_row = jnp.broadcast_to(wrep_sc[pl.ds(t, 1), :], (SUBLANES, wrep_sc.shape[1]))
        dm = jnp.where(_diag_mask(w_row.shape), w_row, 0.0)
        y = jnp.dot(_stack_split_bf16(dm), tiles, preferred_element_type=F32)
        out_ref[0, t] = x_ref[0, t] + gate * (y[:SUBLANES] + y[SUBLANES:])
        return carry

    lax.fori_loop(0, tb, token, 0)


def _peer_v(ids, w, xc, gate, table, expand, n_ctx, tb):
    b, nk, t = ids.shape
    d = xc.shape[-1]
    chunks = d // LANES
    n_ctx_blocks = n_ctx // tb
    out = pl.pallas_call(
        _peer_v_kernel,
        grid=(b, t // tb),
        in_specs=[pl.BlockSpec((1, nk, tb), lambda bi, ti: (bi, 0, ti), memory_space=pltpu.SMEM),
                  pl.BlockSpec((1, tb, nk), lambda bi, ti: (bi, ti, 0)),
                  pl.BlockSpec((1, tb, chunks, LANES), lambda bi, ti: (bi, ti, 0, 0)),
                  pl.BlockSpec((1, 1, chunks, LANES),
                               lambda bi, ti: (bi, (ti >= n_ctx_blocks).astype(jnp.int32), 0, 0)),
                  pl.BlockSpec(memory_space=pltpu.VMEM),
                  pl.BlockSpec(expand.shape, lambda bi, ti: (0, 0))],
        out_specs=pl.BlockSpec((1, tb, chunks, LANES), lambda bi, ti: (bi, ti, 0, 0)),
        out_shape=jax.ShapeDtypeStruct((b, t, chunks, LANES), F32),
        scratch_shapes=[pltpu.VMEM((nk * _ROWS_PER_EXPERT, LANES), jnp.uint32),
                        pltpu.VMEM((tb, nk * SUBLANES), F32)],
        compiler_params=_params(("arbitrary", "arbitrary")),
        name="peer_expert_out",
    )(ids, w, xc.reshape(b, t, chunks, LANES), gate.reshape(b, 2, chunks, LANES), table, expand)
    return out.reshape(b, t, d)


def _pack_table(tab):
    e, d = tab.shape
    t = tab.astype(BF16).reshape(e, d // (2 * LANES), 2, LANES)
    t = jnp.swapaxes(t, -1, -2)
    return lax.bitcast_convert_type(t, jnp.uint32).reshape(e * _ROWS_PER_EXPERT, LANES)


def _peer(xc, mod, norm_g, w_query, sub_keys, table_u, table_v, group, n_ctx):
    shift, scale, gate = mod[3], mod[4], mod[5]
    hn, ids, gates = _route(xc, shift, scale, norm_g, w_query, sub_keys, n_ctx, min(TOK_BLOCK, n_ctx))
    tb = min(PEER_TOK_BLOCK, n_ctx)
    w = _peer_u(ids, hn, gates, table_u, group, tb)
    return _peer_v(ids, w, xc, gate, table_v, group.T, n_ctx, tb)


_DEINTERLEAVE = np.concatenate([np.arange(0, HEAD_DIM, 2), np.arange(1, HEAD_DIM, 2)])


def _rope_tables(n_ctx, seq):
    rows = seq // GRID_W
    row = jnp.repeat(jnp.arange(rows, dtype=F32), GRID_W)
    col = jnp.tile(jnp.arange(GRID_W, dtype=F32), rows)
    half = HEAD_DIM // 2
    inv = ROPE_THETA ** (-jnp.arange(0, half, 2, dtype=F32) / half)
    ang = jnp.concatenate([row[:, None] * inv, col[:, None] * inv], axis=-1)
    cos, sin = jnp.cos(ang), jnp.sin(ang)
    cos_h = jnp.concatenate([cos, cos], axis=-1)
    sin_h = jnp.concatenate([-sin, sin], axis=-1)
    cos_t = jnp.concatenate([jnp.ones((n_ctx, HEAD_DIM), F32), cos_h], axis=0)
    sin_t = jnp.concatenate([jnp.zeros((n_ctx, HEAD_DIM), F32), sin_h], axis=0)
    return jnp.tile(cos_t, (1, 2)), jnp.tile(sin_t, (1, 2))


def _even_columns():
    cols, c0 = [], 0
    for width, rotate in ((A_HEADS, True), (A_KV_HEADS, True), (A_KV_HEADS, False),
                          (B_HEADS, True), (B_KV_HEADS, True), (B_KV_HEADS, False)):
        for _ in range(width):
            cols.append(c0 + (_DEINTERLEAVE if rotate else np.arange(HEAD_DIM)))
            c0 += HEAD_DIM
    return np.concatenate(cols)


def _pair_gain(g):
    return jnp.tile(g[_DEINTERLEAVE], 2)


def _mod_rows(m, b, d):
    lat = m[:b].reshape(b, 6, d)
    ctx = jnp.broadcast_to(m[b].reshape(1, 6, d), (b, 6, d))
    both = jnp.stack([ctx, lat], axis=1)
    return [both[:, :, i][:, :, None, :] for i in range(6)]


def kernel(x, c, ctx, c_ctx, mod_w, mod_b, mix_norm, ffn_norm, even_w_in, even_w_out, a_q_norm, a_k_norm, b_q_norm, b_k_norm, b_sink, mla_w_in, mla_q_a_norm, mla_kv_a_norm, mla_w_uq, mla_w_ukv, mla_q_nope_norm, mla_q_rope_norm, mla_k_nope_norm, mla_k_rope_norm, mla_w_out, peer_w_query, peer_sub_keys, peer_u, peer_v):
    b, seq, d = x.shape
    n_ctx = ctx.shape[1]
    depth = mod_w.shape[0]
    assert b + 1 <= SUBLANES and d == 2 * _ROWS_PER_EXPERT * LANES
    tb = min(TOK_BLOCK, n_ctx)

    xc = jnp.concatenate([ctx, x], axis=1)
    cc = jnp.concatenate([c, c_ctx[None], jnp.zeros((SUBLANES - b - 1, d), F32)], axis=0)
    mods = _mods(cc, mod_w, mod_b)
    cos, sin = _rope_tables(n_ctx, seq)
    nk = PEER_HEADS * PEER_TOPK
    group = jnp.asarray(np.repeat(np.eye(nk, dtype=np.float32), SUBLANES, axis=0), BF16)

    for l in range(depth):
        mod = _mod_rows(mods[l], b, d)
        norm_g = mix_norm[l][None]
        if l % 2 == 0:
            e = l // 2
            w_in = even_w_in[e][:, _even_columns()].astype(BF16)
            gains = jnp.stack([_pair_gain(a_q_norm[e]), _pair_gain(a_k_norm[e]),
                               _pair_gain(b_q_norm[e]), _pair_gain(b_k_norm[e])])
            aq, ak, av, bq, bk, bv = _even_in(xc, mod[0], mod[1], norm_g, w_in, gains, cos, sin, n_ctx, tb)
            oa = _global_attn(aq, ak, av, n_ctx, ATTN_TQ_GQA, ATTN_TK)
            ob = _window_attn(bq, bk, bv, b_sink[e], n_ctx, ATTN_TQ_GQA)
            w_out = even_w_out[e].astype(BF16)
            a_w = A_HEADS * HEAD_DIM
            xc = _out_proj(xc, mod[2], [oa, ob], [w_out[:a_w], w_out[a_w:]], n_ctx, tb)
        else:
            o = l // 2
            pe0 = C_Q_LORA + C_KV_LORA
            w_in = mla_w_in[o]
            w_in = jnp.concatenate([w_in[:, :pe0], w_in[:, pe0:][:, _DEINTERLEAVE],
                                    jnp.zeros((d, LANES - C_ROPE), F32)], axis=1).astype(BF16)
            w_uq = mla_w_uq[o].reshape(C_Q_LORA, C_HEADS, C_NOPE + C_ROPE)
            w_uq = jnp.concatenate([w_uq[:, :, :C_NOPE].reshape(C_Q_LORA, -1),
                                    w_uq[:, :, C_NOPE:][:, :, _DEINTERLEAVE].reshape(C_Q_LORA, -1)],
                                   axis=1).astype(BF16)
            gains = jnp.stack([mla_q_nope_norm[o], _pair_gain(mla_q_rope_norm[o]),
                               mla_k_nope_norm[o], _pair_gain(mla_k_rope_norm[o])])
            q, k, v = _mla_in(xc, mod[0], mod[1], norm_g, w_in, mla_q_a_norm[o][None], mla_kv_a_norm[o][None],
                              w_uq, mla_w_ukv[o].astype(BF16), gains, cos, sin, n_ctx, tb)
            oc = _global_attn(q, k, v, n_ctx, ATTN_TQ_MLA, ATTN_TK)
            xc = _out_proj(xc, mod[2], [oc], [mla_w_out[o].astype(BF16)], n_ctx, tb)
        xc = _peer(xc, mod, ffn_norm[l][None], peer_w_query[l].astype(BF16), peer_sub_keys[l].astype(BF16),
                   _pack_table(peer_u[l]), _pack_table(peer_v[l]), group, n_ctx)
    return xc[:, n_ctx:]
```

```python
import functools
import math

import numpy as np
import jax
import jax.numpy as jnp
from jax import lax
from jax.experimental import pallas as pl
from jax.experimental.pallas import tpu as pltpu

F32 = jnp.float32
BF16 = jnp.bfloat16

GRID_W = 64
ROPE_THETA = 10000.0
NORM_EPS = 1e-6
NEG = -1e30
LOG2E = math.log2(math.e)
HEAD_DIM = 64
A_HEADS, A_KV_HEADS = 8, 2
B_HEADS, B_KV_HEADS = 8, 2
WINDOW = 128
C_HEADS, C_Q_LORA, C_KV_LORA = 8, 384, 256
C_NOPE, C_ROPE, C_V = 128, 64, 128
PEER_HEADS, PEER_KEYS, PEER_TOPK = 8, 128, 16
PEER_EXPERTS = PEER_KEYS * PEER_KEYS

LANES = 128
SUBLANES = 8
VMEM_LIMIT_BYTES = 56 * 1024 * 1024

TOK_BLOCK = 256
PEER_TOK_BLOCK = 128
PEER_TOK_UNROLL = 4
ATTN_TQ_GQA = 256
ATTN_TQ_MLA = 1024
ATTN_TK = 768
WIN_BLOCK = 128

_CAND_COUNTS = tuple(PEER_TOPK // (a + 1) for a in range(PEER_TOPK))
_ROWS_PER_EXPERT = 4


def _params(semantics):
    return pltpu.CompilerParams(dimension_semantics=semantics, vmem_limit_bytes=VMEM_LIMIT_BYTES)


def _rms(x, gain):
    ms = jnp.mean(x * x, axis=-1, keepdims=True)
    return x * lax.rsqrt(ms + NORM_EPS) * gain


def _lane_iota(shape):
    return lax.broadcasted_iota(jnp.int32, shape, len(shape) - 1)


def _head_norm_rope(y, gain, cos, sin):
    lane = _lane_iota(y.shape)
    first = lane < HEAD_DIM
    y2 = y * y
    s_lo = jnp.sum(jnp.where(first, y2, 0.0), axis=-1, keepdims=True)
    s_hi = jnp.sum(jnp.where(first, 0.0, y2), axis=-1, keepdims=True)
    ms = jnp.where(first, s_lo, s_hi) * (1.0 / HEAD_DIM)
    yn = y * lax.rsqrt(ms + NORM_EPS) * gain
    half = HEAD_DIM // 2
    partner = jnp.where((lane % HEAD_DIM) < half,
                        pltpu.roll(yn, LANES - half, 1), pltpu.roll(yn, half, 1))
    return yn * cos + partner * sin


def _mods_kernel(c_ref, w_ref, b_ref, o_ref):
    c = c_ref[...]
    s = c * jax.nn.sigmoid(c)
    o_ref[0] = jnp.dot(s.astype(BF16), w_ref[0].astype(BF16), preferred_element_type=F32) + b_ref[0]


def _mods(cc, mod_w, mod_b):
    depth, d, n = mod_w.shape
    tn = 1536
    return pl.pallas_call(
        _mods_kernel,
        grid=(depth, n // tn),
        in_specs=[pl.BlockSpec((SUBLANES, d), lambda l, j: (0, 0)),
                  pl.BlockSpec((1, d, tn), lambda l, j: (l, 0, j)),
                  pl.BlockSpec((1, 1, tn), lambda l, j: (l, 0, j))],
        out_specs=pl.BlockSpec((1, SUBLANES, tn), lambda l, j: (l, 0, j)),
        out_shape=jax.ShapeDtypeStruct((depth, SUBLANES, n), F32),
        compiler_params=_params(("arbitrary", "arbitrary")),
        name="adaln_mods",
    )(cc, mod_w, mod_b.reshape(depth, 1, n))


def _tok_block(target, n_ctx):
    return min(target, n_ctx) if n_ctx else target


def _mod_spec(d, n_ctx_blocks):
    return pl.BlockSpec((1, 1, 1, d), lambda b, t, *_: (b, (t >= n_ctx_blocks).astype(jnp.int32), 0, 0))


def _even_in_kernel(x_ref, shift_ref, scale_ref, g_ref, w_ref, gains_ref, cos_ref, sin_ref,
                    aq_ref, ak_ref, av_ref, bq_ref, bk_ref, bv_ref, *, q_scale):
    hn = _rms(x_ref[0], g_ref[...]) * (1.0 + scale_ref[0, 0]) + shift_ref[0, 0]
    y = jnp.dot(hn.astype(BF16), w_ref[...], preferred_element_type=F32)
    cos, sin = cos_ref[...], sin_ref[...]

    def heads(col0, n_heads, gain_row, out_ref, scale):
        for j in range(n_heads // 2):
            c = col0 + j * LANES
            o = _head_norm_rope(y[:, c:c + LANES], gains_ref[gain_row:gain_row + 1, :], cos, sin) * scale
            out_ref[0, 2 * j] = o[:, :HEAD_DIM].astype(out_ref.dtype)
            out_ref[0, 2 * j + 1] = o[:, HEAD_DIM:].astype(out_ref.dtype)

    def values(col0, n_heads, out_ref):
        for h in range(n_heads):
            c = col0 + h * HEAD_DIM
            out_ref[0, h] = y[:, c:c + HEAD_DIM].astype(out_ref.dtype)

    a_q, a_kv = A_HEADS * HEAD_DIM, A_KV_HEADS * HEAD_DIM
    b_q, b_kv = B_HEADS * HEAD_DIM, B_KV_HEADS * HEAD_DIM
    c0 = 0
    heads(c0, A_HEADS, 0, aq_ref, q_scale); c0 += a_q
    heads(c0, A_KV_HEADS, 1, ak_ref, 1.0); c0 += a_kv
    values(c0, A_KV_HEADS, av_ref); c0 += a_kv
    heads(c0, B_HEADS, 2, bq_ref, q_scale); c0 += b_q
    heads(c0, B_KV_HEADS, 3, bk_ref, 1.0); c0 += b_kv
    values(c0, B_KV_HEADS, bv_ref)


def _even_in(xc, shift, scale, norm_g, w_in, gains, cos, sin, n_ctx, tb):
    b, t, d = xc.shape
    n_ctx_blocks = n_ctx // tb
    tok = lambda bi, ti: (bi, ti, 0)
    head_spec = lambda n: pl.BlockSpec((1, n, tb, HEAD_DIM), lambda bi, ti: (bi, 0, ti, 0))
    head_shape = lambda n: jax.ShapeDtypeStruct((b, n, t, HEAD_DIM), BF16)
    return pl.pallas_call(
        functools.partial(_even_in_kernel, q_scale=HEAD_DIM ** -0.5 * LOG2E),
        grid=(b, t // tb),
        in_specs=[pl.BlockSpec((1, tb, d), tok),
                  _mod_spec(d, n_ctx_blocks), _mod_spec(d, n_ctx_blocks),
                  pl.BlockSpec((1, d), lambda bi, ti: (0, 0)),
                  pl.BlockSpec(w_in.shape, lambda bi, ti: (0, 0)),
                  pl.BlockSpec(gains.shape, lambda bi, ti: (0, 0)),
                  pl.BlockSpec((tb, LANES), lambda bi, ti: (ti, 0)),
                  pl.BlockSpec((tb, LANES), lambda bi, ti: (ti, 0))],
        out_specs=[head_spec(A_HEADS), head_spec(A_KV_HEADS), head_spec(A_KV_HEADS),
                   head_spec(B_HEADS), head_spec(B_KV_HEADS), head_spec(B_KV_HEADS)],
        out_shape=[head_shape(A_HEADS), head_shape(A_KV_HEADS), head_shape(A_KV_HEADS),
                   head_shape(B_HEADS), head_shape(B_KV_HEADS), head_shape(B_KV_HEADS)],
        compiler_params=_params(("parallel", "parallel")),
        name="even_in_proj",
    )(xc, shift, scale, norm_g, w_in, gains, cos, sin)


def _mla_in_kernel(x_ref, shift_ref, scale_ref, g_ref, w_in_ref, qa_g_ref, kva_g_ref, w_uq_ref, w_ukv_ref,
                   gains_ref, cos_ref, sin_ref, q_ref, k_ref, v_ref, *, q_scale):
    hn = _rms(x_ref[0], g_ref[...]) * (1.0 + scale_ref[0, 0]) + shift_ref[0, 0]
    p = jnp.dot(hn.astype(BF16), w_in_ref[...], preferred_element_type=F32)
    cq = _rms(p[:, :C_Q_LORA], qa_g_ref[...])
    ckv = _rms(p[:, C_Q_LORA:C_Q_LORA + C_KV_LORA], kva_g_ref[...])
    cos, sin = cos_ref[...], sin_ref[...]
    lane = _lane_iota(cos.shape)
    first = lane < C_ROPE
    q_nope_g, q_rope_g = gains_ref[0:1, :], gains_ref[1:2, :]
    k_nope_g, k_rope_g = gains_ref[2:3, :], gains_ref[3:4, :]

    c_pe = C_Q_LORA + C_KV_LORA
    k_pe = _head_norm_rope(p[:, c_pe:c_pe + LANES], k_rope_g, cos, sin)
    k_pe = jnp.where(first, k_pe, 0.0).astype(k_ref.dtype)

    q = jnp.dot(cq.astype(BF16), w_uq_ref[...], preferred_element_type=F32)
    kv = jnp.dot(ckv.astype(BF16), w_ukv_ref[...], preferred_element_type=F32)
    rope0 = C_HEADS * C_NOPE
    for j in range(C_HEADS // 2):
        r = _head_norm_rope(q[:, rope0 + j * LANES:rope0 + (j + 1) * LANES], q_rope_g, cos, sin) * q_scale
        for h, rr in ((2 * j, r), (2 * j + 1, pltpu.roll(r, C_ROPE, 1))):
            qn = _rms(q[:, h * C_NOPE:(h + 1) * C_NOPE], q_nope_g) * q_scale
            q_ref[0, h, :, :C_NOPE] = qn.astype(q_ref.dtype)
            q_ref[0, h, :, C_NOPE:] = jnp.where(first, rr, 0.0).astype(q_ref.dtype)
    for h in range(C_HEADS):
        c = h * (C_NOPE + C_V)
        k_ref[0, h, :, :C_NOPE] = _rms(kv[:, c:c + C_NOPE], k_nope_g).astype(k_ref.dtype)
        k_ref[0, h, :, C_NOPE:] = k_pe
        v_ref[0, h] = kv[:, c + C_NOPE:c + C_NOPE + C_V].astype(v_ref.dtype)


def _mla_in(xc, shift, scale, norm_g, w_in, qa_g, kva_g, w_uq, w_ukv, gains, cos, sin, n_ctx, tb):
    b, t, d = xc.shape
    n_ctx_blocks = n_ctx // tb
    full = lambda a: pl.BlockSpec(a.shape, lambda bi, ti: (0,) * a.ndim)
    dqk = C_NOPE + LANES
    hspec = lambda w: pl.BlockSpec((1, C_HEADS, tb, w), lambda bi, ti: (bi, 0, ti, 0))
    hshape = lambda w: jax.ShapeDtypeStruct((b, C_HEADS, t, w), BF16)
    return pl.pallas_call(
        functools.partial(_mla_in_kernel, q_scale=(C_NOPE + C_ROPE) ** -0.5 * LOG2E),
        grid=(b, t // tb),
        in_specs=[pl.BlockSpec((1, tb, d), lambda bi, ti: (bi, ti, 0)),
                  _mod_spec(d, n_ctx_blocks), _mod_spec(d, n_ctx_blocks),
                  full(norm_g), full(w_in), full(qa_g), full(kva_g), full(w_uq), full(w_ukv), full(gains),
                  pl.BlockSpec((tb, LANES), lambda bi, ti: (ti, 0)),
                  pl.BlockSpec((tb, LANES), lambda bi, ti: (ti, 0))],
        out_specs=[hspec(dqk), hspec(dqk), hspec(C_V)],
        out_shape=[hshape(dqk), hshape(dqk), hshape(C_V)],
        compiler_params=_params(("parallel", "parallel")),
        name="mla_in_proj",
    )(xc, shift, scale, norm_g, w_in, qa_g, kva_g, w_uq, w_ukv, gains, cos, sin)


def _softmax_step(s, v, m_sc, l_sc, acc_sc):
    m_prev = m_sc[...]
    m_new = jnp.maximum(m_prev, jnp.max(s, axis=-1, keepdims=True))
    alpha = jnp.exp2(m_prev - m_new)
    p = jnp.exp2(s - m_new)
    l_sc[...] = alpha * l_sc[...] + jnp.sum(p, axis=-1, keepdims=True)
    acc_sc[...] = alpha * acc_sc[...] + jnp.dot(p.astype(v.dtype), v, preferred_element_type=F32)
    m_sc[...] = m_new


def _scores(q_ref, k_ref):
    g, tq, dq = q_ref.shape[1:]
    q = q_ref[0].reshape(g * tq, dq)
    return lax.dot_general(q, k_ref[0, 0], (((1,), (1,)), ((), ())), preferred_element_type=F32)


def _write_heads(o_ref, l_sc, acc_sc, g, tq, dv):
    out = acc_sc[...] / l_sc[...]
    for h in range(g):
        o_ref[0, :, h * dv:(h + 1) * dv] = out[h * tq:(h + 1) * tq].astype(o_ref.dtype)


def _global_attn_kernel(q_ref, k_ref, v_ref, o_ref, m_sc, l_sc, acc_sc, *, n_ctx, ctx_rows, tq, tk):
    qi, ki = pl.program_id(2), pl.program_id(3)
    g, dv = q_ref.shape[1], v_ref.shape[-1]

    @pl.when(ki == 0)
    def _():
        m_sc[...] = jnp.full(m_sc.shape, NEG, F32)
        l_sc[...] = jnp.zeros(l_sc.shape, F32)
        acc_sc[...] = jnp.zeros(acc_sc.shape, F32)

    ctx_q = qi * tq < ctx_rows

    @pl.when(jnp.logical_not(ctx_q))
    def _():
        _softmax_step(_scores(q_ref, k_ref), v_ref[0, 0], m_sc, l_sc, acc_sc)

    @pl.when(jnp.logical_and(ctx_q, ki * tk < n_ctx))
    def _():
        s = _scores(q_ref, k_ref)
        kpos = ki * tk + _lane_iota(s.shape)
        _softmax_step(jnp.where(kpos < n_ctx, s, NEG), v_ref[0, 0], m_sc, l_sc, acc_sc)

    @pl.when(ki == pl.num_programs(3) - 1)
    def _():
        _write_heads(o_ref, l_sc, acc_sc, g, tq, dv)


def _global_attn(q, k, v, n_ctx, ctx_rows, tq, tk):
    b, hq, nq, dq = q.shape
    hkv, t, dv = k.shape[1], k.shape[2], v.shape[-1]
    g = hq // hkv
    tq = min(tq, ctx_rows) if ctx_rows else min(tq, nq)
    tk = max(c for c in range(LANES, tk + 1, LANES) if t % c == 0)
    assert ctx_rows % tq == 0 and nq % tq == 0
    return pl.pallas_call(
        functools.partial(_global_attn_kernel, n_ctx=n_ctx, ctx_rows=ctx_rows, tq=tq, tk=tk),
        grid=(b, hkv, nq // tq, t // tk),
        in_specs=[pl.BlockSpec((1, g, tq, dq), lambda bi, hi, qi, ki: (bi, hi, qi, 0)),
                  pl.BlockSpec((1, 1, tk, dq), lambda bi, hi, qi, ki: (bi, hi, ki, 0)),
                  pl.BlockSpec((1, 1, tk, dv), lambda bi, hi, qi, ki: (bi, hi, ki, 0))],
        out_specs=pl.BlockSpec((1, tq, g * dv), lambda bi, hi, qi, ki: (bi, qi, hi)),
        out_shape=jax.ShapeDtypeStruct((b, nq, hq * dv), BF16),
        scratch_shapes=[pltpu.VMEM((g * tq, 1), F32), pltpu.VMEM((g * tq, 1), F32),
                        pltpu.VMEM((g * tq, dv), F32)],
        compiler_params=_params(("parallel", "parallel", "parallel", "arbitrary")),
        name="global_attention",
    )(q, k, v)


def _window_attn_kernel(sink_ref, q_ref, k_ref, v_ref, o_ref, m_sc, l_sc, acc_sc, *, n_ctx, tq, n_blocks):
    hi, qi, j = pl.program_id(1), pl.program_id(2), pl.program_id(3)
    g, dv = q_ref.shape[1], v_ref.shape[-1]
    n_ctx_blocks = n_ctx // WIN_BLOCK
    q_blocks = tq // WIN_BLOCK

    @pl.when(j == 0)
    def _():
        for h in range(g):
            m_sc[h * tq:(h + 1) * tq, :] = jnp.full((tq, 1), sink_ref[hi * g + h] * LOG2E, F32)
        l_sc[...] = jnp.ones(l_sc.shape, F32)
        acc_sc[...] = jnp.zeros(acc_sc.shape, F32)

    @pl.when(j < n_ctx_blocks)
    def _():
        _softmax_step(_scores(q_ref, k_ref), v_ref[0, 0], m_sc, l_sc, acc_sc)

    kb = qi * q_blocks - 1 + (j - n_ctx_blocks)
    lat_q = qi * tq >= n_ctx
    in_range = jnp.logical_and(kb >= n_ctx_blocks, kb < n_blocks)

    @pl.when(jnp.logical_and(j >= n_ctx_blocks, jnp.logical_and(lat_q, in_range)))
    def _():
        s = _scores(q_ref, k_ref)
        row = lax.broadcasted_iota(jnp.int32, s.shape, 0)
        qpos = qi * tq + row % tq
        kpos = kb * WIN_BLOCK + _lane_iota(s.shape)
        _softmax_step(jnp.where(jnp.abs(kpos - qpos) <= WINDOW, s, NEG), v_ref[0, 0], m_sc, l_sc, acc_sc)

    @pl.when(j == pl.num_programs(3) - 1)
    def _():
        _write_heads(o_ref, l_sc, acc_sc, g, tq, dv)


def _window_attn(q, k, v, sink, n_ctx, tq):
    b, hq, t, dq = q.shape
    hkv, dv = k.shape[1], v.shape[-1]
    g = hq // hkv
    tq = min(tq, n_ctx)
    assert n_ctx % tq == 0 and t % tq == 0 and tq % WIN_BLOCK == 0 and n_ctx % WIN_BLOCK == 0
    n_blocks = t // WIN_BLOCK
    n_ctx_blocks = n_ctx // WIN_BLOCK
    q_blocks = tq // WIN_BLOCK
    n_steps = n_ctx_blocks + q_blocks + 2

    def kv_index(bi, hi, qi, j):
        kb = jnp.where(j < n_ctx_blocks, j, qi * q_blocks - 1 + (j - n_ctx_blocks))
        return (bi, hi, jnp.clip(kb, 0, n_blocks - 1), 0)

    return pl.pallas_call(
        functools.partial(_window_attn_kernel, n_ctx=n_ctx, tq=tq, n_blocks=n_blocks),
        grid=(b, hkv, t // tq, n_steps),
        in_specs=[pl.BlockSpec(memory_space=pltpu.SMEM),
                  pl.BlockSpec((1, g, tq, dq), lambda bi, hi, qi, j: (bi, hi, qi, 0)),
                  pl.BlockSpec((1, 1, WIN_BLOCK, dq), kv_index),
                  pl.BlockSpec((1, 1, WIN_BLOCK, dv), kv_index)],
        out_specs=pl.BlockSpec((1, tq, g * dv), lambda bi, hi, qi, j: (bi, qi, hi)),
        out_shape=jax.ShapeDtypeStruct((b, t, hq * dv), BF16),
        scratch_shapes=[pltpu.VMEM((g * tq, 1), F32), pltpu.VMEM((g * tq, 1), F32),
                        pltpu.VMEM((g * tq, dv), F32)],
        compiler_params=_params(("parallel", "parallel", "parallel", "arbitrary")),
        name="window_attention",
    )(sink, q, k, v)


def _out_proj_kernel(*refs, n_in):
    x_ref, gate_ref = refs[0], refs[1]
    o_refs, w_refs = refs[2:2 + n_in], refs[2 + n_in:2 + 2 * n_in]
    out_ref = refs[2 + 2 * n_in]
    acc = jnp.dot(o_refs[0][0], w_refs[0][...], preferred_element_type=F32)
    for o_ref, w_ref in zip(o_refs[1:], w_refs[1:]):
        acc += jnp.dot(o_ref[0], w_ref[...], preferred_element_type=F32)
    out_ref[0] = x_ref[0] + gate_ref[0, 0] * acc


def _out_proj(xc, gate, outs, weights, n_ctx, tb, skip_rows=0):
    b, _, d = xc.shape
    t = outs[0].shape[1]
    n_in = len(outs)
    skip = skip_rows // tb
    assert skip_rows % tb == 0
    tok = lambda bi, ti: (bi, ti, 0)
    return pl.pallas_call(
        functools.partial(_out_proj_kernel, n_in=n_in),
        grid=(b, t // tb),
        in_specs=[pl.BlockSpec((1, tb, d), lambda bi, ti: (bi, ti + skip, 0)), _mod_spec(d, n_ctx // tb)]
        + [pl.BlockSpec((1, tb, o.shape[-1]), tok) for o in outs]
        + [pl.BlockSpec(w.shape, lambda bi, ti: (0, 0)) for w in weights],
        out_specs=pl.BlockSpec((1, tb, d), tok),
        out_shape=jax.ShapeDtypeStruct((b, t, d), F32),
        compiler_params=_params(("parallel", "parallel")),
        name="out_proj_residual",
    )(xc, gate, *outs, *weights)


def _top_rows(s, k):
    n, width = s.shape
    rows = lax.broadcasted_iota(jnp.int32, s.shape, 0)
    slot = lax.broadcasted_iota(jnp.int32, (k, width), 0)
    vals = jnp.zeros((k, width), F32)
    ids = jnp.zeros((k, width), jnp.int32)
    for r in range(k):
        m = jnp.max(s, axis=0, keepdims=True)
        i = jnp.min(jnp.where(s == m, rows, n), axis=0, keepdims=True)
        vals = jnp.where(slot == r, m, vals)
        ids = jnp.where(slot == r, i, ids)
        s = jnp.where(rows == i, -jnp.inf, s)
    return vals, ids


def _route_kernel(x_ref, shift_ref, scale_ref, g_ref, wq_ref, keys_ref, hn_ref, ids_ref, gate_ref,
                  hn_sc, ids_sc, gate_sc):
    h = pl.program_id(2)

    @pl.when(h == 0)
    def _():
        hn = _rms(x_ref[0], g_ref[...]) * (1.0 + scale_ref[0, 0]) + shift_ref[0, 0]
        hn_ref[0] = hn
        hn_sc[...] = hn.astype(BF16)

    q = jnp.dot(hn_sc[...], wq_ref[...], preferred_element_type=F32)
    tops = []
    for p in range(2):
        qp = q[:, p * PEER_KEYS:(p + 1) * PEER_KEYS].astype(BF16)
        s_t = lax.dot_general(keys_ref[0, p], qp, (((1,), (1,)), ((), ())),
                              preferred_element_type=F32)
        tops.append(_top_rows(s_t, PEER_TOPK))
    (s0, i0), (s1, i1) = tops
    cand, cand_e = [], []
    sub = lax.broadcasted_iota(jnp.int32, (SUBLANES, s0.shape[1]), 0)
    for a, nb in enumerate(_CAND_COUNTS):
        if nb == 1:
            continue
        for b0 in range(0, nb, SUBLANES):
            live = sub < nb - b0
            cand.append(jnp.where(live, s0[a:a + 1] + s1[b0:b0 + SUBLANES], -jnp.inf))
            cand_e.append(i0[a:a + 1] * PEER_KEYS + i1[b0:b0 + SUBLANES])
    single = _CAND_COUNTS.index(1)
    cand.append(s0[single:] + s1[0:1])
    cand_e.append(i0[single:] * PEER_KEYS + i1[0:1])
    cand = jnp.concatenate(cand, axis=0)
    cand_e = jnp.concatenate(cand_e, axis=0)
    best, pos = _top_rows(cand, PEER_TOPK)
    rows = lax.broadcasted_iota(jnp.int32, cand.shape, 0)
    slot = lax.broadcasted_iota(jnp.int32, best.shape, 0)
    expert = jnp.zeros(best.shape, jnp.int32)
    for r in range(PEER_TOPK):
        picked = jnp.sum(jnp.where(rows == pos[r:r + 1], cand_e, 0), axis=0, keepdims=True)
        expert = jnp.where(slot == r, picked, expert)
    e = jnp.exp(best - best[0:1])
    rows16 = pl.ds(pl.multiple_of(h * PEER_TOPK, PEER_TOPK), PEER_TOPK)
    gate_sc[rows16, :] = e / jnp.sum(e, axis=0, keepdims=True)
    ids_sc[rows16, :] = expert * _ROWS_PER_EXPERT

    @pl.when(h == pl.num_programs(2) - 1)
    def _():
        gate_ref[0] = gate_sc[...].T
        ids_ref[0] = ids_sc[...].T


def _route(xc, shift, scale, norm_g, w_query, sub_keys, n_ctx, tb):
    b, t, d = xc.shape
    nk = PEER_HEADS * PEER_TOPK
    return pl.pallas_call(
        _route_kernel,
        grid=(b, t // tb, PEER_HEADS),
        in_specs=[pl.BlockSpec((1, tb, d), lambda bi, ti, h: (bi, ti, 0)),
                  _mod_spec(d, n_ctx // tb), _mod_spec(d, n_ctx // tb),
                  pl.BlockSpec((1, d), lambda bi, ti, h: (0, 0)),
                  pl.BlockSpec((d, 2 * PEER_KEYS), lambda bi, ti, h: (0, h)),
                  pl.BlockSpec((1, 2, PEER_KEYS, PEER_KEYS), lambda bi, ti, h: (h, 0, 0, 0))],
        out_specs=[pl.BlockSpec((1, tb, d), lambda bi, ti, h: (bi, ti, 0)),
                   pl.BlockSpec((1, tb, nk), lambda bi, ti, h: (bi, ti, 0)),
                   pl.BlockSpec((1, tb, nk), lambda bi, ti, h: (bi, ti, 0))],
        out_shape=[jax.ShapeDtypeStruct((b, t, d), F32),
                   jax.ShapeDtypeStruct((b, t, nk), jnp.int32),
                   jax.ShapeDtypeStruct((b, t, nk), F32)],
        scratch_shapes=[pltpu.VMEM((tb, d), BF16), pltpu.VMEM((nk, tb), jnp.int32),
                        pltpu.VMEM((nk, tb), F32)],
        compiler_params=_params(("parallel", "parallel", "arbitrary")),
        name="peer_route",
    )(xc, shift, scale, norm_g, w_query, sub_keys)


def _gather_tiles(ids_ref, t, table_ref, tiles_ref):
    for k in range(ids_ref.shape[2]):
        r = pl.multiple_of(ids_ref[0, t, k], _ROWS_PER_EXPERT)
        tiles_ref[k * _ROWS_PER_EXPERT:(k + 1) * _ROWS_PER_EXPERT, :] = table_ref[pl.ds(r, _ROWS_PER_EXPERT), :]
    return pltpu.bitcast(tiles_ref[...], BF16)


def _split_bf16(x):
    hi = x.astype(BF16)
    return hi, (x - hi.astype(F32)).astype(BF16)


def _stack_split_bf16(x):
    hi = x.astype(BF16).astype(F32)
    return jnp.concatenate([hi, x - hi], axis=0).astype(BF16)


def _diag_mask(shape):
    row = lax.broadcasted_iota(jnp.int32, shape, 0)
    return (row % SUBLANES) == (_lane_iota(shape) % SUBLANES)


def _peer_u_kernel(ids_ref, hn_ref, gate_ref, table_ref, group_ref, w_ref, *scratch):
    tile_scs, dots_sc = scratch[:-1], scratch[-1]
    tb = hn_ref.shape[1]

    def tokens(i, carry):
        for j, tiles_sc in enumerate(tile_scs):
            t = i * len(tile_scs) + j
            tiles = _gather_tiles(ids_ref, t, table_ref, tiles_sc)
            xt = _stack_split_bf16(hn_ref[0, t])
            s = lax.dot_general(xt, tiles, (((1,), (1,)), ((), ())), preferred_element_type=F32)
            s = s[:SUBLANES] + s[SUBLANES:]
            dots_sc[pl.ds(t, 1), :] = jnp.sum(jnp.where(_diag_mask(s.shape), s, 0.0), axis=0, keepdims=True)
        return carry

    lax.fori_loop(0, tb // len(tile_scs), tokens, 0)
    hi, lo = _split_bf16(dots_sc[...])
    a = (jnp.dot(hi, group_ref[...], preferred_element_type=F32)
         + jnp.dot(lo, group_ref[...], preferred_element_type=F32))
    act = 0.5 * a * (1.0 + lax.erf(a * (2.0 ** -0.5)))
    w_ref[0] = act * gate_ref[0]


def _peer_u(ids, hn, gates, table, group, tb):
    b, t, nk = ids.shape
    d = hn.shape[-1]
    chunks = d // LANES
    tok = lambda bi, ti: (bi, ti, 0)
    return pl.pallas_call(
        _peer_u_kernel,
        grid=(b, t // tb),
        in_specs=[pl.BlockSpec((1, tb, nk), tok, memory_space=pltpu.SMEM),
                  pl.BlockSpec((1, tb, chunks, LANES), lambda bi, ti: (bi, ti, 0, 0)),
                  pl.BlockSpec((1, tb, nk), tok),
                  pl.BlockSpec(memory_space=pltpu.VMEM),
                  pl.BlockSpec(group.shape, lambda bi, ti: (0, 0))],
        out_specs=pl.BlockSpec((1, tb, nk), tok),
        out_shape=jax.ShapeDtypeStruct((b, t, nk), F32),
        scratch_shapes=[pltpu.VMEM((nk * _ROWS_PER_EXPERT, LANES), jnp.uint32)] * PEER_TOK_UNROLL
        + [pltpu.VMEM((tb, nk * SUBLANES), F32)],
        compiler_params=_params(("arbitrary", "arbitrary")),
        name="peer_expert_in",
    )(ids, hn.reshape(b, t, chunks, LANES), gates, table, group)


def _peer_v_kernel(ids_ref, w_ref, x_ref, gate_ref, table_ref, expand_ref, out_ref, *scratch):
    tile_scs, wrep_sc = scratch[:-1], scratch[-1]
    tb = w_ref.shape[1]
    hi, lo = _split_bf16(w_ref[0])
    wrep_sc[...] = (jnp.dot(hi, expand_ref[...], preferred_element_type=F32)
                    + jnp.dot(lo, expand_ref[...], preferred_element_type=F32))
    gate = gate_ref[0, 0]

    def tokens(i, carry):
        for j, tiles_sc in enumerate(tile_scs):
            t = i * len(tile_scs) + j
            tiles = _gather_tiles(ids_ref, t, table_ref, tiles_sc)
            w_row = jnp.broadcast_to(wrep_sc[pl.ds(t, 1), :], (SUBLANES, wrep_sc.shape[1]))
            dm = jnp.where(_diag_mask(w_row.shape), w_row, 0.0)
            y = jnp.dot(_stack_split_bf16(dm), tiles, preferred_element_type=F32)
            out_ref[0, t] = x_ref[0, t] + gate * (y[:SUBLANES] + y[SUBLANES:])
        return carry

    lax.fori_loop(0, tb // len(tile_scs), tokens, 0)


def _peer_v(ids, w, xc, gate, table, expand, n_ctx, tb):
    b, t, nk = ids.shape
    d = xc.shape[-1]
    chunks = d // LANES
    n_ctx_blocks = n_ctx // tb
    tok = lambda bi, ti: (bi, ti, 0)
    out = pl.pallas_call(
        _peer_v_kernel,
        grid=(b, t // tb),
        in_specs=[pl.BlockSpec((1, tb, nk), tok, memory_space=pltpu.SMEM),
                  pl.BlockSpec((1, tb, nk), tok),
                  pl.BlockSpec((1, tb, chunks, LANES), lambda bi, ti: (bi, ti, 0, 0)),
                  pl.BlockSpec((1, 1, chunks, LANES),
                               lambda bi, ti: (bi, (ti >= n_ctx_blocks).astype(jnp.int32), 0, 0)),
                  pl.BlockSpec(memory_space=pltpu.VMEM),
                  pl.BlockSpec(expand.shape, lambda bi, ti: (0, 0))],
        out_specs=pl.BlockSpec((1, tb, chunks, LANES), lambda bi, ti: (bi, ti, 0, 0)),
        out_shape=jax.ShapeDtypeStruct((b, t, chunks, LANES), F32),
        scratch_shapes=[pltpu.VMEM((nk * _ROWS_PER_EXPERT, LANES), jnp.uint32)] * PEER_TOK_UNROLL
        + [pltpu.VMEM((tb, nk * SUBLANES), F32)],
        compiler_params=_params(("arbitrary", "arbitrary")),
        name="peer_expert_out",
    )(ids, w, xc.reshape(b, t, chunks, LANES), gate.reshape(b, 2, chunks, LANES), table, expand)
    return out.reshape(b, t, d)


def _pack_table(tab):
    e, d = tab.shape
    t = tab.astype(BF16).reshape(e, d // (2 * LANES), 2, LANES)
    t = jnp.swapaxes(t, -1, -2)
    return lax.bitcast_convert_type(t, jnp.uint32).reshape(e * _ROWS_PER_EXPERT, LANES)


def _peer(xc, mod, norm_g, w_query, sub_keys, table_u, table_v, group, n_ctx):
    shift, scale, gate = mod[3], mod[4], mod[5]
    hn, ids, gates = _route(xc, shift, scale, norm_g, w_query, sub_keys, n_ctx, _tok_block(TOK_BLOCK, n_ctx))
    tb = _tok_block(PEER_TOK_BLOCK, n_ctx)
    w = _peer_u(ids, hn, gates, table_u, group, tb)
    return _peer_v(ids, w, xc, gate, table_v, group.T, n_ctx, tb)


_DEINTERLEAVE = np.concatenate([np.arange(0, HEAD_DIM, 2), np.arange(1, HEAD_DIM, 2)])


def _rope_tables(n_ctx, seq):
    rows = seq // GRID_W
    row = jnp.repeat(jnp.arange(rows, dtype=F32), GRID_W)
    col = jnp.tile(jnp.arange(GRID_W, dtype=F32), rows)
    half = HEAD_DIM // 2
    inv = ROPE_THETA ** (-jnp.arange(0, half, 2, dtype=F32) / half)
    ang = jnp.concatenate([row[:, None] * inv, col[:, None] * inv], axis=-1)
    cos, sin = jnp.cos(ang), jnp.sin(ang)
    cos_h = jnp.concatenate([cos, cos], axis=-1)
    sin_h = jnp.concatenate([-sin, sin], axis=-1)
    cos_t = jnp.concatenate([jnp.ones((n_ctx, HEAD_DIM), F32), cos_h], axis=0)
    sin_t = jnp.concatenate([jnp.zeros((n_ctx, HEAD_DIM), F32), sin_h], axis=0)
    return jnp.tile(cos_t, (1, 2)), jnp.tile(sin_t, (1, 2))


def _even_columns():
    cols, c0 = [], 0
    for width, rotate in ((A_HEADS, True), (A_KV_HEADS, True), (A_KV_HEADS, False),
                          (B_HEADS, True), (B_KV_HEADS, True), (B_KV_HEADS, False)):
        for _ in range(width):
            cols.append(c0 + (_DEINTERLEAVE if rotate else np.arange(HEAD_DIM)))
            c0 += HEAD_DIM
    return np.concatenate(cols)


def _pair_gain(g):
    return jnp.tile(g[_DEINTERLEAVE], 2)


def _mod_rows(m, b, d):
    lat = m[:b].reshape(b, 6, d)
    ctx = jnp.broadcast_to(m[b].reshape(1, 6, d), (b, 6, d))
    both = jnp.stack([ctx, lat], axis=1)
    return [both[:, :, i][:, :, None, :] for i in range(6)]


def kernel(x, c, ctx, c_ctx, mod_w, mod_b, mix_norm, ffn_norm, even_w_in, even_w_out, a_q_norm, a_k_norm, b_q_norm, b_k_norm, b_sink, mla_w_in, mla_q_a_norm, mla_kv_a_norm, mla_w_uq, mla_w_ukv, mla_q_nope_norm, mla_q_rope_norm, mla_k_nope_norm, mla_k_rope_norm, mla_w_out, peer_w_query, peer_sub_keys, peer_u, peer_v):
    b, seq, d = x.shape
    n_ctx = ctx.shape[1]
    depth = mod_w.shape[0]
    assert b + 1 <= SUBLANES and d == 2 * _ROWS_PER_EXPERT * LANES
    tb = _tok_block(TOK_BLOCK, n_ctx)

    xc = jnp.concatenate([ctx, x], axis=1)
    cc = jnp.concatenate([c, c_ctx[None], jnp.zeros((SUBLANES - b - 1, d), F32)], axis=0)
    mods = _mods(cc, mod_w, mod_b)
    cos, sin = _rope_tables(n_ctx, seq)
    nk = PEER_HEADS * PEER_TOPK
    group = jnp.asarray(np.repeat(np.eye(nk, dtype=np.float32), SUBLANES, axis=0), BF16)

    for l in range(depth):
        mod = _mod_rows(mods[l], b, d)
        norm_g = mix_norm[l][None]
        q_rows, n_keep = (slice(n_ctx, None), 0) if l == depth - 1 else (slice(None), n_ctx)
        if l % 2 == 0:
            e = l // 2
            w_in = even_w_in[e][:, _even_columns()].astype(BF16)
            gains = jnp.stack([_pair_gain(a_q_norm[e]), _pair_gain(a_k_norm[e]),
                               _pair_gain(b_q_norm[e]), _pair_gain(b_k_norm[e])])
            aq, ak, av, bq, bk, bv = _even_in(xc, mod[0], mod[1], norm_g, w_in, gains, cos, sin, n_ctx, tb)
            oa = _global_attn(aq[:, :, q_rows], ak, av, n_ctx, n_keep, ATTN_TQ_GQA, ATTN_TK)
            ob = _window_attn(bq, bk, bv, b_sink[e], n_ctx, ATTN_TQ_GQA)[:, q_rows]
            w_out = even_w_out[e].astype(BF16)
            a_w = A_HEADS * HEAD_DIM
            xc = _out_proj(xc, mod[2], [oa, ob], [w_out[:a_w], w_out[a_w:]], n_keep, tb, n_ctx - n_keep)
        else:
            o = l // 2
            pe0 = C_Q_LORA + C_KV_LORA
            w_in = mla_w_in[o]
            w_in = jnp.concatenate([w_in[:, :pe0], w_in[:, pe0:][:, _DEINTERLEAVE],
                                    jnp.zeros((d, LANES - C_ROPE), F32)], axis=1).astype(BF16)
            w_uq = mla_w_uq[o].reshape(C_Q_LORA, C_HEADS, C_NOPE + C_ROPE)
            w_uq = jnp.concatenate([w_uq[:, :, :C_NOPE].reshape(C_Q_LORA, -1),
                                    w_uq[:, :, C_NOPE:][:, :, _DEINTERLEAVE].reshape(C_Q_LORA, -1)],
                                   axis=1).astype(BF16)
            gains = jnp.stack([mla_q_nope_norm[o], _pair_gain(mla_q_rope_norm[o]),
                               mla_k_nope_norm[o], _pair_gain(mla_k_rope_norm[o])])
            q, k, v = _mla_in(xc, mod[0], mod[1], norm_g, w_in, mla_q_a_norm[o][None], mla_kv_a_norm[o][None],
                              w_uq, mla_w_ukv[o].astype(BF16), gains, cos, sin, n_ctx, tb)
            oc = _global_attn(q[:, :, q_rows], k, v, n_ctx, n_keep, ATTN_TQ_MLA, ATTN_TK)
            xc = _out_proj(xc, mod[2], [oc], [mla_w_out[o].astype(BF16)], n_keep, tb, n_ctx - n_keep)
        xc = _peer(xc, mod, ffn_norm[l][None], peer_w_query[l].astype(BF16), peer_sub_keys[l].astype(BF16),
                   _pack_table(peer_u[l]), _pack_table(peer_v[l]), group, n_keep)
    return xc
```
